```python
import math
import jax
import jax.numpy as jnp
from jax import lax
import numpy as np

D_MODEL = 2048
BATCH = 4
SEQ = 4096
DEPTH = 1

CHUNK = 64
Q_BLOCK = 128
DIFF_HEADS = 8
DIFF_HEAD_DIM = 64
DIFF_V_DIM = 2 * DIFF_HEAD_DIM
DSA_HEADS = 8
DSA_HEAD_DIM = 128
DSA_LATENT = 512
IDX_HEADS = 16
IDX_DIM = 64
DSA_TOPK_MAX = 256
MEM_TOKENS = 256
MEM_HEADS = 4
MEM_HEAD_DIM = 128
D_FF = 11 * D_MODEL // 4
BRANCH_A_WIDTH = DIFF_HEADS * DIFF_V_DIM
BRANCH_B_WIDTH = DSA_HEADS * DSA_HEAD_DIM
IN_WIDTHS = (DIFF_HEADS * 2 * DIFF_HEAD_DIM, DIFF_HEADS * 2 * DIFF_HEAD_DIM, DIFF_HEADS * DIFF_V_DIM,
             DSA_HEADS * DSA_HEAD_DIM, DSA_LATENT, IDX_HEADS * IDX_DIM, IDX_DIM, IDX_HEADS)
IN_TOTAL = sum(IN_WIDTHS)
ALPHA = (2.0 * DEPTH) ** 0.25
BETA = (8.0 * DEPTH) ** -0.25
LN_EPS = 1e-5

kernel_name = 'hybrid_diff_dsa_macaron_deepnorm_layer'


def _layer_norm(x, g, b):
    xf = x.astype(jnp.float32)
    mu = jnp.mean(xf, axis=-1, keepdims=True)
    var = jnp.mean(jnp.square(xf - mu), axis=-1, keepdims=True)
    return ((xf - mu) * lax.rsqrt(var + LN_EPS) * g.astype(jnp.float32) + b.astype(jnp.float32)).astype(x.dtype)


def _rms_norm(x, g):
    xf = x.astype(jnp.float32)
    return (xf * lax.rsqrt(jnp.mean(jnp.square(xf), axis=-1, keepdims=True) + LN_EPS) * g.astype(jnp.float32)).astype(x.dtype)


def _alibi_slopes(n):
    return 2.0 ** (-8.0 * jnp.arange(1, n + 1, dtype=jnp.float32) / n)


def _swiglu(x, w_gate, w_up, w_down):
    return (jax.nn.silu(x @ w_gate) * (x @ w_up)) @ w_down


def _chunk_admissible(q_pos, k_pos):
    return (k_pos // CHUNK)[None, :] <= (q_pos // CHUNK)[:, None]


def _diff_attention(q, k, v, lam, sub_g, lambda_init):
    seq = q.shape[1]
    slopes = _alibi_slopes(DIFF_HEADS)
    scale = DIFF_HEAD_DIM ** -0.5
    pos = jnp.arange(seq, dtype=jnp.int32)
    outs = []
    for start in range(0, seq, Q_BLOCK):
        end = start + Q_BLOCK
        q_pos, k_pos = pos[start:end], pos[:end]
        logits = jnp.einsum('bthcd,bshcd->bhcts', q[:, start:end], k[:, :end]).astype(jnp.float32) * scale
        dist = jnp.abs(q_pos[:, None] - k_pos[None, :]).astype(jnp.float32)
        logits = logits - (slopes[:, None, None] * dist)[None, :, None]
        logits = jnp.where(_chunk_admissible(q_pos, k_pos), logits, -jnp.inf)
        p = jax.nn.softmax(logits, axis=-1)
        attn = p[:, :, 0] - lam * p[:, :, 1]
        outs.append(jnp.einsum('bhts,bshe->bthe', attn.astype(v.dtype), v[:, :end]))
    o = jnp.concatenate(outs, axis=1)
    o = _rms_norm(o, sub_g) * (1.0 - lambda_init)
    return o.reshape(o.shape[0], seq, BRANCH_A_WIDTH)


def _dsa_attention(q, c, iq, ik, iw, w_uk, w_uv, kv_g):
    batch, seq = q.shape[0], q.shape[1]
    top_k = min(DSA_TOPK_MAX, seq // 4)
    slopes = _alibi_slopes(DSA_HEADS)
    scale = DSA_HEAD_DIM ** -0.5
    idx_scale = (IDX_HEADS * IDX_DIM) ** -0.5
    c = _rms_norm(c, kv_g)
    q_lat = jnp.einsum('bthd,hcd->bthc', q, w_uk)
    k_pos = jnp.arange(seq, dtype=jnp.int32)
    gather = jax.vmap(lambda cb, ib: cb[ib])

    def one_block(start):
        q_pos = start + jnp.arange(Q_BLOCK, dtype=jnp.int32)
        iq_b = lax.dynamic_slice_in_dim(iq, start, Q_BLOCK, axis=1)
        iw_b = lax.dynamic_slice_in_dim(iw, start, Q_BLOCK, axis=1)
        ql_b = lax.dynamic_slice_in_dim(q_lat, start, Q_BLOCK, axis=1)
        head_scores = jax.nn.relu(jnp.einsum('bthd,bsd->bths', iq_b, ik))
        score = jnp.einsum('bths,bth->bts', head_scores, iw_b).astype(jnp.float32) * idx_scale
        score = jnp.where(_chunk_admissible(q_pos, k_pos)[None], score, -jnp.inf)
        _, sel = lax.top_k(score, top_k)
        c_sel = gather(c, sel)
        valid = (sel // CHUNK) <= (q_pos // CHUNK)[None, :, None]
        dist = jnp.abs(q_pos[None, :, None] - sel).astype(jnp.float32)
        logits = jnp.einsum('bthc,btkc->bthk', ql_b, c_sel).astype(jnp.float32) * scale
        logits = logits - slopes[None, None, :, None] * dist[:, :, None, :]
        logits = jnp.where(valid[:, :, None, :], logits, -jnp.inf)
        p = jax.nn.softmax(logits, axis=-1).astype(c.dtype)
        o_lat = jnp.einsum('bthk,btkc->bthc', p, c_sel)
        return jnp.einsum('bthc,hcd->bthd', o_lat, w_uv)

    starts = jnp.arange(0, seq, Q_BLOCK, dtype=jnp.int32)
    o = lax.map(one_block, starts)
    return jnp.moveaxis(o, 0, 1).reshape(batch, seq, BRANCH_B_WIDTH)


def _hybrid_mixer(x, w_in, diff_lambda, diff_subln_g, lambda_init, dsa_kv_g, dsa_w_uk, dsa_w_uv,
                  w_gate, b_gate, w_branch_a, w_branch_b, w_mix_out):
    b, s = x.shape[0], x.shape[1]
    h = x @ w_in
    aq, ak, av, bq, bc, iq, ik, iw = jnp.split(h, np.cumsum(IN_WIDTHS)[:-1].tolist(), axis=-1)
    lf = diff_lambda.astype(jnp.float32)
    lam = jnp.exp(jnp.sum(lf[0] * lf[1])) - jnp.exp(jnp.sum(lf[2] * lf[3])) + lambda_init
    y_a = _diff_attention(aq.reshape(b, s, DIFF_HEADS, 2, DIFF_HEAD_DIM),
                          ak.reshape(b, s, DIFF_HEADS, 2, DIFF_HEAD_DIM),
                          av.reshape(b, s, DIFF_HEADS, DIFF_V_DIM), lam, diff_subln_g, lambda_init)
    y_b = _dsa_attention(bq.reshape(b, s, DSA_HEADS, DSA_HEAD_DIM), bc,
                         iq.reshape(b, s, IDX_HEADS, IDX_DIM), ik, iw, dsa_w_uk, dsa_w_uv, dsa_kv_g)
    gates = jax.nn.sigmoid(jnp.einsum('bsd,gde->gbse', x, w_gate) + b_gate[:, None, None, :])
    merged = gates[0] * (y_a @ w_branch_a) + gates[1] * (y_b @ w_branch_b)
    return merged @ w_mix_out


def _memory_attention(x, mem, w_q, w_kv, w_o):
    b, s = x.shape[0], x.shape[1]
    q = (x @ w_q).reshape(b, s, MEM_HEADS, MEM_HEAD_DIM)
    kv = (mem @ w_kv).reshape(b, mem.shape[1], 2, MEM_HEADS, MEM_HEAD_DIM)
    logits = jnp.einsum('bthd,bmhd->bhtm', q, kv[:, :, 0]).astype(jnp.float32) * MEM_HEAD_DIM ** -0.5
    p = jax.nn.softmax(logits, axis=-1).astype(x.dtype)
    o = jnp.einsum('bhtm,bmhd->bthd', p, kv[:, :, 1])
    return o.reshape(b, s, MEM_HEADS * MEM_HEAD_DIM) @ w_o


def setup_inputs(seed: int = 0) -> dict:
    key = jax.random.key(seed)
    ks = iter(jax.random.split(key, 32))
    f32 = jnp.float32
    L = DEPTH

    def w(shape, fan_in, gain=1.0):
        return jax.random.normal(next(ks), shape, f32) * (gain * fan_in ** -0.5)

    def noise(shape, s):
        return jax.random.normal(next(ks), shape, f32) * s

    return {
        'x': jax.random.normal(next(ks), (BATCH, SEQ, D_MODEL), f32),
        'mem': jax.random.normal(next(ks), (BATCH, MEM_TOKENS, D_MODEL), f32),
        'ffn1_w_gate': w((L, D_MODEL, D_FF), D_MODEL),
        'ffn1_w_up': w((L, D_MODEL, D_FF), D_MODEL),
        'ffn1_w_down': w((L, D_FF, D_MODEL), D_FF, BETA),
        'w_in': w((L, D_MODEL, IN_TOTAL), D_MODEL),
        'diff_lambda': noise((L, 4, DIFF_HEAD_DIM), 0.1),
        'diff_subln_g': 1.0 + noise((L, DIFF_V_DIM), 0.02),
        'dsa_kv_g': 1.0 + noise((L, DSA_LATENT), 0.02),
        'dsa_w_uk': w((L, DSA_HEADS, DSA_LATENT, DSA_HEAD_DIM), DSA_LATENT),
        'dsa_w_uv': w((L, DSA_HEADS, DSA_LATENT, DSA_HEAD_DIM), DSA_LATENT),
        'w_gate': w((L, 2, D_MODEL, D_MODEL), D_MODEL),
        'b_gate': noise((L, 2, D_MODEL), 0.02),
        'w_branch_a': w((L, BRANCH_A_WIDTH, D_MODEL), BRANCH_A_WIDTH),
        'w_branch_b': w((L, BRANCH_B_WIDTH, D_MODEL), BRANCH_B_WIDTH),
        'w_mix_out': w((L, D_MODEL, D_MODEL), D_MODEL, BETA),
        'mem_w_q': w((L, D_MODEL, MEM_HEADS * MEM_HEAD_DIM), D_MODEL),
        'mem_w_kv': w((L, D_MODEL, 2 * MEM_HEADS * MEM_HEAD_DIM), D_MODEL),
        'mem_w_o': w((L, MEM_HEADS * MEM_HEAD_DIM, D_MODEL), MEM_HEADS * MEM_HEAD_DIM, BETA),
        'ffn2_w_gate': w((L, D_MODEL, D_FF), D_MODEL),
        'ffn2_w_up': w((L, D_MODEL, D_FF), D_MODEL),
        'ffn2_w_down': w((L, D_FF, D_MODEL), D_FF, BETA),
        'ln_g': 1.0 + noise((L, 4, D_MODEL), 0.02),
        'ln_b': noise((L, 4, D_MODEL), 0.02),
    }


def reference(x, mem, ffn1_w_gate, ffn1_w_up, ffn1_w_down, w_in, diff_lambda, diff_subln_g,
              dsa_kv_g, dsa_w_uk, dsa_w_uv, w_gate, b_gate, w_branch_a, w_branch_b, w_mix_out,
              mem_w_q, mem_w_kv, mem_w_o, ffn2_w_gate, ffn2_w_up, ffn2_w_down, ln_g, ln_b):
    for l in range(DEPTH):
        lambda_init = 0.8 - 0.6 * math.exp(-0.3 * l)
        x = _layer_norm(ALPHA * x + 0.5 * _swiglu(x, ffn1_w_gate[l], ffn1_w_up[l], ffn1_w_down[l]),
                        ln_g[l, 0], ln_b[l, 0])
        x = _layer_norm(ALPHA * x + _hybrid_mixer(x, w_in[l], diff_lambda[l], diff_subln_g[l], lambda_init,
                                                  dsa_kv_g[l], dsa_w_uk[l], dsa_w_uv[l], w_gate[l], b_gate[l],
                                                  w_branch_a[l], w_branch_b[l], w_mix_out[l]),
                        ln_g[l, 1], ln_b[l, 1])
        x = _layer_norm(ALPHA * x + _memory_attention(x, mem, mem_w_q[l], mem_w_kv[l], mem_w_o[l]),
                        ln_g[l, 2], ln_b[l, 2])
        x = _layer_norm(ALPHA * x + 0.5 * _swiglu(x, ffn2_w_gate[l], ffn2_w_up[l], ffn2_w_down[l]),
                        ln_g[l, 3], ln_b[l, 3])
    return x
```

```python
import functools
import math

import jax
import jax.numpy as jnp
from jax import lax
from jax.experimental import pallas as pl
from jax.experimental.pallas import tpu as pltpu

F32 = jnp.float32
BF16 = jnp.bfloat16
I32 = jnp.int32

CHUNK = 64
DIFF_HEADS = 8
DIFF_HEAD_DIM = 64
DIFF_V_DIM = 2 * DIFF_HEAD_DIM
DSA_HEADS = 8
DSA_HEAD_DIM = 128
DSA_LATENT = 512
IDX_HEADS = 16
IDX_DIM = 64
DSA_TOPK_MAX = 256
MEM_HEADS = 4
MEM_HEAD_DIM = 128
LN_EPS = 1e-5
IN_WIDTHS = (DIFF_HEADS * 2 * DIFF_HEAD_DIM, DIFF_HEADS * 2 * DIFF_HEAD_DIM, DIFF_HEADS * DIFF_V_DIM,
             DSA_HEADS * DSA_HEAD_DIM, DSA_LATENT, IDX_HEADS * IDX_DIM, IDX_DIM, IDX_HEADS)

LANES = 128
NEG_BIG = -1e30
INT_MIN = -(2 ** 31)
VMEM_LIMIT = 56 * 1024 * 1024

COL_AQ, COL_AK, COL_AV, COL_BQ, COL_IQ, COL_BC, COL_IK, COL_IW = 0, 1024, 2048, 3072, 4096, 5120, 5632, 5760
IN_PADDED = 6144
IN_TILE = 512

_NT = (((1,), (1,)), ((), ()))


def _dot(a, b):
    return jnp.dot(a, b, preferred_element_type=F32)


def _dot_nt(a, b):
    return lax.dot_general(a, b, _NT, preferred_element_type=F32)


def _layer_norm(y, g, b):
    mu = jnp.mean(y, axis=-1, keepdims=True)
    yc = y - mu
    var = jnp.mean(yc * yc, axis=-1, keepdims=True)
    return yc * lax.rsqrt(var + LN_EPS) * g + b


def _params(*sem):
    return pltpu.CompilerParams(dimension_semantics=sem, vmem_limit_bytes=VMEM_LIMIT)


def _ffn_ln_kernel(x_ref, wg_ref, wu_ref, wd_ref, g_ref, b_ref, o_ref, ob_ref, acc_ref, xb_ref, *, alpha):
    j = pl.program_id(1)

    @pl.when(j == 0)
    def _():
        xb_ref[...] = x_ref[...].astype(BF16)
        acc_ref[...] = jnp.zeros_like(acc_ref)

    xb = xb_ref[...]
    gate = _dot(xb, wg_ref[...])
    up = _dot(xb, wu_ref[...])
    hidden = gate * jax.nn.sigmoid(gate) * up
    acc_ref[...] += _dot(hidden.astype(BF16), wd_ref[...])

    @pl.when(j == pl.num_programs(1) - 1)
    def _():
        y = _layer_norm(alpha * x_ref[...] + 0.5 * acc_ref[...], g_ref[...], b_ref[...])
        o_ref[...] = y
        ob_ref[...] = y.astype(BF16)


def _ffn_ln(x, wg, wu, wd, g, b, *, alpha, tm, tf):
    n, d = x.shape
    dff = wg.shape[1]
    row = pl.BlockSpec((tm, d), lambda i, j: (i, 0))
    vec = pl.BlockSpec((1, d), lambda i, j: (0, 0))
    return pl.pallas_call(
        functools.partial(_ffn_ln_kernel, alpha=alpha),
        grid=(n // tm, dff // tf),
        in_specs=[row,
                  pl.BlockSpec((d, tf), lambda i, j: (0, j)),
                  pl.BlockSpec((d, tf), lambda i, j: (0, j)),
                  pl.BlockSpec((tf, d), lambda i, j: (j, 0)),
                  vec, vec],
        out_specs=[row, row],
        out_shape=[jax.ShapeDtypeStruct((n, d), F32), jax.ShapeDtypeStruct((n, d), BF16)],
        scratch_shapes=[pltpu.VMEM((tm, d), F32), pltpu.VMEM((tm, d), BF16)],
        compiler_params=_params("parallel", "arbitrary"),
        name="ffn_ln",
    )(x, wg, wu, wd, g, b)


def _in_proj_kernel(xb_ref, w_ref, kvg_ref, o_ref, *, latent_block):
    j = pl.program_id(1)
    r = _dot(xb_ref[...], w_ref[...])

    @pl.when(j != latent_block)
    def _():
        o_ref[...] = r.astype(BF16)

    @pl.when(j == latent_block)
    def _():
        ms = jnp.mean(r * r, axis=-1, keepdims=True)
        o_ref[...] = (r * lax.rsqrt(ms + LN_EPS) * kvg_ref[...]).astype(BF16)


def _in_proj(xb, w, kvg, *, tm):
    n, d = xb.shape
    return pl.pallas_call(
        functools.partial(_in_proj_kernel, latent_block=COL_BC // IN_TILE),
        grid=(n // tm, IN_PADDED // IN_TILE),
        in_specs=[pl.BlockSpec((tm, d), lambda i, j: (i, 0)),
                  pl.BlockSpec((d, IN_TILE), lambda i, j: (0, j)),
                  pl.BlockSpec((1, DSA_LATENT), lambda i, j: (0, 0))],
        out_specs=pl.BlockSpec((tm, IN_TILE), lambda i, j: (i, j)),
        out_shape=jax.ShapeDtypeStruct((n, IN_PADDED), BF16),
        compiler_params=_params("parallel", "arbitrary"),
        name="in_proj",
    )(xb, w, kvg)


def _softmax_update(s, v_blk, m_ref, l_ref, acc_ref):
    m_prev = m_ref[...]
    m_new = jnp.maximum(m_prev, jnp.max(s, axis=-1, keepdims=True))
    scale_old = jnp.exp(m_prev - m_new)
    p = jnp.exp(s - m_new)
    l_ref[...] = scale_old * l_ref[...] + jnp.sum(p, axis=-1, keepdims=True)
    acc_ref[...] = scale_old * acc_ref[...] + _dot(p.astype(BF16), v_blk)
    m_ref[...] = m_new


def _diff_attn_kernel(slopes_ref, q_ref, k_ref, v_ref, lam_ref, subg_ref, o_ref,
                      q2_ref, a_ref, m_ref, l_ref, acc_ref, *, tq, lambda_init):
    h = pl.program_id(1)
    qi = pl.program_id(2)
    slope = slopes_ref[h]
    rows = 2 * tq

    row = lax.broadcasted_iota(I32, (rows, tq), 0)
    col = lax.broadcasted_iota(I32, (rows, tq), 1)
    r_in = jnp.where(row >= tq, row - tq, row)

    @pl.when(qi == 0)
    def _():
        a_ref[...] = slope * (r_in - col).astype(F32)

    lane = lax.broadcasted_iota(I32, (tq, LANES), 1)
    qs = q_ref[...] * jnp.asarray(DIFF_HEAD_DIM ** -0.5, BF16)
    zero = jnp.zeros_like(qs)
    q2_ref[0:tq, :] = jnp.where(lane < DIFF_HEAD_DIM, qs, zero)
    q2_ref[tq:rows, :] = jnp.where(lane >= DIFF_HEAD_DIM, qs, zero)
    m_ref[...] = jnp.full_like(m_ref, NEG_BIG)
    l_ref[...] = jnp.zeros_like(l_ref)
    acc_ref[...] = jnp.zeros_like(acc_ref)

    def off_diag(kb, carry):
        s = _dot_nt(q2_ref[...], k_ref[kb])
        delta = ((qi - kb) * tq).astype(F32)
        s = s - a_ref[...] - slope * delta
        _softmax_update(s, v_ref[kb], m_ref, l_ref, acc_ref)
        return carry

    lax.fori_loop(0, qi, off_diag, 0)

    s = _dot_nt(q2_ref[...], k_ref[qi]) - jnp.abs(a_ref[...])
    s = jnp.where((col // CHUNK) <= (r_in // CHUNK), s, NEG_BIG)
    _softmax_update(s, v_ref[qi], m_ref, l_ref, acc_ref)

    lf = lam_ref[...]
    lam = (jnp.exp(jnp.sum(lf[0:1] * lf[1:2], axis=-1, keepdims=True))
           - jnp.exp(jnp.sum(lf[2:3] * lf[3:4], axis=-1, keepdims=True)) + lambda_init)
    o = acc_ref[0:tq, :] / l_ref[0:tq, :] - lam * (acc_ref[tq:rows, :] / l_ref[tq:rows, :])
    o = o * lax.rsqrt(jnp.mean(o * o, axis=-1, keepdims=True) + LN_EPS) * subg_ref[...]
    o_ref[...] = (o * (1.0 - lambda_init)).astype(BF16)


def _diff_attn(h4, slopes, lam, subg, *, tq, lambda_init):
    b, nkb, _, _ = h4.shape
    blk = lambda off: off // LANES
    kv_spec = lambda off: pl.BlockSpec((None, nkb, tq, LANES), lambda bi, h, i: (bi, 0, 0, blk(off) + h))
    return pl.pallas_call(
        functools.partial(_diff_attn_kernel, tq=tq, lambda_init=lambda_init),
        grid=(b, DIFF_HEADS, nkb),
        in_specs=[pl.BlockSpec(memory_space=pltpu.SMEM),
                  pl.BlockSpec((None, None, tq, LANES), lambda bi, h, i: (bi, i, 0, blk(COL_AQ) + h)),
                  kv_spec(COL_AK), kv_spec(COL_AV),
                  pl.BlockSpec((4, DIFF_HEAD_DIM), lambda bi, h, i: (0, 0)),
                  pl.BlockSpec((1, DIFF_V_DIM), lambda bi, h, i: (0, 0))],
        out_specs=pl.BlockSpec((None, None, tq, LANES), lambda bi, h, i: (bi, i, 0, h)),
        out_shape=jax.ShapeDtypeStruct((b, nkb, tq, DIFF_HEADS * DIFF_V_DIM), BF16),
        scratch_shapes=[pltpu.VMEM((2 * tq, LANES), BF16),
                        pltpu.VMEM((2 * tq, tq), F32),
                        pltpu.VMEM((2 * tq, 1), F32),
                        pltpu.VMEM((2 * tq, 1), F32),
                        pltpu.VMEM((2 * tq, DIFF_V_DIM), F32)],
        compiler_params=_params("arbitrary", "arbitrary", "arbitrary"),
        name="diff_attn",
    )(slopes, h4, h4, h4, lam, subg)


def _dsa_kernel(slopes_ref, bq_ref, iq_ref, iw_ref, ik_ref, c_ref, wuk_ref, wuv_ref, o_ref,
                lhs_ref, iwb_ref, keys_ref, qlat_ref, a_ref, scol_ref, m_ref, l_ref, acc_ref,
                *, tq, top_k):
    first = jnp.logical_and(pl.program_id(0) == 0, pl.program_id(1) == 0)
    qi = pl.program_id(1)
    nsub = tq // LANES
    rows = DSA_HEADS * tq

    row = lax.broadcasted_iota(I32, (tq, tq), 0)
    col = lax.broadcasted_iota(I32, (tq, tq), 1)
    admissible = (col // CHUNK) <= (row // CHUNK)

    @pl.when(first)
    def _():
        rel = (row - col).astype(F32)
        for h in range(DSA_HEADS):
            a_ref[h * tq:(h + 1) * tq, :] = slopes_ref[h] * rel
            scol_ref[h * tq:(h + 1) * tq, :] = jnp.full((tq, 1), slopes_ref[h], F32)

    lane = lax.broadcasted_iota(I32, (tq, LANES), 1)
    for h in range(IDX_HEADS):
        blk = iq_ref[:, (h // 2) * LANES:(h // 2 + 1) * LANES]
        keep = (lane < IDX_DIM) if h % 2 == 0 else (lane >= IDX_DIM)
        lhs_ref[h * tq:(h + 1) * tq, :] = jnp.where(keep, blk, jnp.zeros_like(blk))
        iwb_ref[h * tq:(h + 1) * tq, :] = jnp.broadcast_to(iw_ref[:, h:h + 1].astype(F32), (tq, LANES))

    idx_scale = (IDX_HEADS * IDX_DIM) ** -0.5

    def score_keys(kb):
        s = _dot_nt(lhs_ref[...], ik_ref[kb])
        iwb = iwb_ref[...]
        if nsub > 1:
            iwb = jnp.concatenate([iwb] * nsub, axis=1)
        r = jnp.maximum(s, 0.0) * iwb
        score = r[0:tq]
        for h in range(1, IDX_HEADS):
            score = score + r[h * tq:(h + 1) * tq]
        score = score * idx_scale
        bits = pltpu.bitcast(score, I32)
        return bits ^ ((bits >> 31) & 0x7FFFFFFF)

    def score_body(kb, carry):
        keys_ref[kb] = score_keys(kb)
        return carry

    lax.fori_loop(0, qi, score_body, 0)
    keys_ref[qi] = jnp.where(admissible, score_keys(qi), INT_MIN)

    def bisect(i, thr):
        cand = thr ^ jnp.left_shift(jnp.int32(1), 31 - i)

        def count(kb, acc):
            kblk = keys_ref[kb]
            for sidx in range(nsub):
                acc = acc + (kblk[:, sidx * LANES:(sidx + 1) * LANES] >= cand).astype(I32)
            return acc

        acc = lax.fori_loop(0, qi + 1, count, jnp.zeros((tq, LANES), I32))
        cnt = jnp.sum(acc, axis=-1, keepdims=True)
        return jnp.where(cnt >= top_k, cand, thr)

    thr = lax.fori_loop(0, 32, bisect, jnp.full((tq, LANES), INT_MIN, I32))
    thr = jnp.maximum(thr, INT_MIN + 1)
    if nsub > 1:
        thr = jnp.concatenate([thr] * nsub, axis=1)

    for h in range(DSA_HEADS):
        ql = _dot(bq_ref[:, h * DSA_HEAD_DIM:(h + 1) * DSA_HEAD_DIM], wuk_ref[h])
        qlat_ref[h * tq:(h + 1) * tq, :] = ql.astype(BF16)
    m_ref[...] = jnp.full_like(m_ref, NEG_BIG)
    l_ref[...] = jnp.zeros_like(l_ref)
    acc_ref[...] = jnp.zeros_like(acc_ref)
    scale = DSA_HEAD_DIM ** -0.5

    def logits(kb):
        c_blk = c_ref[kb]
        s = _dot_nt(qlat_ref[...], c_blk) * scale
        bias = jnp.where(keys_ref[kb] >= thr, 0.0, NEG_BIG)
        return s + jnp.concatenate([bias] * DSA_HEADS, axis=0), c_blk

    def attn_body(kb, carry):
        s, c_blk = logits(kb)
        delta = ((qi - kb) * tq).astype(F32)
        s = s - a_ref[...] - scol_ref[...] * delta
        _softmax_update(s, c_blk, m_ref, l_ref, acc_ref)
        return carry

    lax.fori_loop(0, qi, attn_body, 0)
    s, c_blk = logits(qi)
    _softmax_update(s - jnp.abs(a_ref[...]), c_blk, m_ref, l_ref, acc_ref)

    for h in range(DSA_HEADS):
        o_lat = acc_ref[h * tq:(h + 1) * tq, :] / l_ref[h * tq:(h + 1) * tq, :]
        o_ref[:, h * DSA_HEAD_DIM:(h + 1) * DSA_HEAD_DIM] = _dot(o_lat.astype(BF16), wuv_ref[h]).astype(BF16)


def _dsa_attn(h4, slopes, wuk_t, wuv, *, tq, top_k):
    b, nkb, _, _ = h4.shape
    wide = DSA_HEADS * DSA_HEAD_DIM
    rows = DSA_HEADS * tq
    q_spec = lambda off: pl.BlockSpec((None, None, tq, wide), lambda bi, i: (bi, i, 0, off // wide))
    return pl.pallas_call(
        functools.partial(_dsa_kernel, tq=tq, top_k=top_k),
        grid=(b, nkb),
        in_specs=[pl.BlockSpec(memory_space=pltpu.SMEM),
                  q_spec(COL_BQ), q_spec(COL_IQ),
                  pl.BlockSpec((None, None, tq, LANES), lambda bi, i: (bi, i, 0, COL_IW // LANES)),
                  pl.BlockSpec((None, nkb, tq, LANES), lambda bi, i: (bi, 0, 0, COL_IK // LANES)),
                  pl.BlockSpec((None, nkb, tq, DSA_LATENT), lambda bi, i: (bi, 0, 0, COL_BC // DSA_LATENT)),
                  pl.BlockSpec((DSA_HEADS, DSA_HEAD_DIM, DSA_LATENT), lambda bi, i: (0, 0, 0)),
                  pl.BlockSpec((DSA_HEADS, DSA_LATENT, DSA_HEAD_DIM), lambda bi, i: (0, 0, 0))],
        out_specs=pl.BlockSpec((None, None, tq, wide), lambda bi, i: (bi, i, 0, 0)),
        out_shape=jax.ShapeDtypeStruct((b, nkb, tq, wide), BF16),
        scratch_shapes=[pltpu.VMEM((IDX_HEADS * tq, LANES), BF16),
                        pltpu.VMEM((IDX_HEADS * tq, LANES), F32),
                        pltpu.VMEM((nkb, tq, tq), I32),
                        pltpu.VMEM((rows, DSA_LATENT), BF16),
                        pltpu.VMEM((rows, tq), F32),
                        pltpu.VMEM((rows, 1), F32),
                        pltpu.VMEM((rows, 1), F32),
                        pltpu.VMEM((rows, 1), F32),
                        pltpu.VMEM((rows, DSA_LATENT), F32)],
        compiler_params=_params("arbitrary", "arbitrary"),
        name="dsa_attn",
    )(slopes, h4, h4, h4, h4, h4, wuk_t, wuv)


def _mix_ln_kernel(xb_ref, x_ref, ya_ref, yb_ref, wg0_ref, wg1_ref, bg_ref, wa_ref, wb_ref, wm_ref,
                   g_ref, b_ref, o_ref, ob_ref, acc_ref, *, alpha):
    j = pl.program_id(1)

    @pl.when(j == 0)
    def _():
        acc_ref[...] = jnp.zeros_like(acc_ref)

    xb = xb_ref[...]
    gate_a = jax.nn.sigmoid(_dot(xb, wg0_ref[...]) + bg_ref[0:1, :])
    gate_b = jax.nn.sigmoid(_dot(xb, wg1_ref[...]) + bg_ref[1:2, :])
    merged = gate_a * _dot(ya_ref[...], wa_ref[...]) + gate_b * _dot(yb_ref[...], wb_ref[...])
    acc_ref[...] += _dot(merged.astype(BF16), wm_ref[...])

    @pl.when(j == pl.num_programs(1) - 1)
    def _():
        y = _layer_norm(alpha * x_ref[...] + acc_ref[...], g_ref[...], b_ref[...])
        o_ref[...] = y
        ob_ref[...] = y.astype(BF16)


def _mix_ln(xb, x, ya, yb, wgate, bgate, wa, wb, wm, g, b, *, alpha, tm, tn):
    n, d = x.shape
    wa_w, wb_w = ya.shape[1], yb.shape[1]
    row = pl.BlockSpec((tm, d), lambda i, j: (i, 0))
    vec = pl.BlockSpec((1, d), lambda i, j: (0, 0))
    return pl.pallas_call(
        functools.partial(_mix_ln_kernel, alpha=alpha),
        grid=(n // tm, d // tn),
        in_specs=[row, row,
                  pl.BlockSpec((tm, wa_w), lambda i, j: (i, 0)),
                  pl.BlockSpec((tm, wb_w), lambda i, j: (i, 0)),
                  pl.BlockSpec((None, d, tn), lambda i, j: (0, 0, j)),
                  pl.BlockSpec((None, d, tn), lambda i, j: (1, 0, j)),
                  pl.BlockSpec((2, tn), lambda i, j: (0, j)),
                  pl.BlockSpec((wa_w, tn), lambda i, j: (0, j)),
                  pl.BlockSpec((wb_w, tn), lambda i, j: (0, j)),
                  pl.BlockSpec((tn, d), lambda i, j: (j, 0)),
                  vec, vec],
        out_specs=[row, row],
        out_shape=[jax.ShapeDtypeStruct((n, d), F32), jax.ShapeDtypeStruct((n, d), BF16)],
        scratch_shapes=[pltpu.VMEM((tm, d), F32)],
        compiler_params=_params("parallel", "arbitrary"),
        name="mix_ln",
    )(xb, x, ya, yb, wgate, wgate, bgate, wa, wb, wm, g, b)


def _mem_kv_kernel(mem_ref, w_ref, o_ref):
    o_ref[...] = _dot(mem_ref[...].astype(BF16), w_ref[...]).astype(BF16)


def _mem_kv(mem, w):
    b, m, d = mem.shape
    width = w.shape[1]
    return pl.pallas_call(
        _mem_kv_kernel,
        grid=(b,),
        in_specs=[pl.BlockSpec((None, m, d), lambda bi: (bi, 0, 0)),
                  pl.BlockSpec((d, width), lambda bi: (0, 0))],
        out_specs=pl.BlockSpec((None, m, width), lambda bi: (bi, 0, 0)),
        out_shape=jax.ShapeDtypeStruct((b, m, width), BF16),
        compiler_params=_params("parallel"),
        name="mem_kv",
    )(mem, w)


def _mem_ln_kernel(xb_ref, x_ref, kv_ref, wq_ref, wo_ref, g_ref, b_ref, o_ref, ob_ref, *, alpha):
    q = _dot(xb_ref[...], wq_ref[...]).astype(BF16)
    width = MEM_HEADS * MEM_HEAD_DIM
    scale = MEM_HEAD_DIM ** -0.5
    heads = []
    for h in range(MEM_HEADS):
        lo, hi = h * MEM_HEAD_DIM, (h + 1) * MEM_HEAD_DIM
        s = _dot_nt(q[:, lo:hi], kv_ref[:, lo:hi]) * scale
        p = jnp.exp(s - jnp.max(s, axis=-1, keepdims=True))
        p = p / jnp.sum(p, axis=-1, keepdims=True)
        heads.append(_dot(p.astype(BF16), kv_ref[:, width + lo:width + hi]))
    o = jnp.concatenate(heads, axis=1).astype(BF16)
    y = _layer_norm(alpha * x_ref[...] + _dot(o, wo_ref[...]), g_ref[...], b_ref[...])
    o_ref[...] = y
    ob_ref[...] = y.astype(BF16)


def _mem_ln(xb3, x3, kv, wq, wo, g, b, *, alpha, tm):
    bsz, s, d = x3.shape
    m, kvw = kv.shape[1], kv.shape[2]
    width = wq.shape[1]
    row = pl.BlockSpec((None, tm, d), lambda bi, i: (bi, i, 0))
    vec = pl.BlockSpec((1, d), lambda bi, i: (0, 0))
    return pl.pallas_call(
        functools.partial(_mem_ln_kernel, alpha=alpha),
        grid=(bsz, s // tm),
        in_specs=[row, row,
                  pl.BlockSpec((None, m, kvw), lambda bi, i: (bi, 0, 0)),
                  pl.BlockSpec((d, width), lambda bi, i: (0, 0)),
                  pl.BlockSpec((width, d), lambda bi, i: (0, 0)),
                  vec, vec],
        out_specs=[row, row],
        out_shape=[jax.ShapeDtypeStruct((bsz, s, d), F32), jax.ShapeDtypeStruct((bsz, s, d), BF16)],
        compiler_params=_params("parallel", "parallel"),
        name="mem_ln",
    )(xb3, x3, kv, wq, wo, g, b)


def _pick(n, pref):
    t = min(n, pref)
    while n % t:
        t //= 2
    return t


def _alibi_slopes(n):
    return 2.0 ** (-8.0 * jnp.arange(1, n + 1, dtype=F32) / n)


def _arrange_w_in(w):
    bounds = [sum(IN_WIDTHS[:i + 1]) for i in range(len(IN_WIDTHS) - 1)]
    aq, ak, av, bq, bc, iq, ik, iw = jnp.split(w, bounds, axis=-1)
    pad = jnp.zeros((w.shape[0], IN_PADDED - COL_IW - IDX_HEADS), w.dtype)
    return jnp.concatenate([aq, ak, av, bq, iq, bc, ik, ik, iw, pad], axis=-1).astype(BF16)


def kernel(x, mem, ffn1_w_gate, ffn1_w_up, ffn1_w_down, w_in, diff_lambda, diff_subln_g, dsa_kv_g, dsa_w_uk,
           dsa_w_uv, w_gate, b_gate, w_branch_a, w_branch_b, w_mix_out, mem_w_q, mem_w_kv, mem_w_o,
           ffn2_w_gate, ffn2_w_up, ffn2_w_down, ln_g, ln_b):
    bsz, seq, d = x.shape
    depth = w_in.shape[0]
    n = bsz * seq
    alpha = (2.0 * depth) ** 0.25
    top_k = min(DSA_TOPK_MAX, seq // 4)
    tm = _pick(n, 512)
    tf = _pick(ffn1_w_gate.shape[2], 512)
    tq_diff = _pick(seq, 256)
    tq_dsa = _pick(seq, 128)
    tm_mem = _pick(seq, 512)
    bf = lambda a: a.astype(BF16)
    vec = lambda a: a.reshape(1, -1)

    xf = x.reshape(n, d)
    for l in range(depth):
        lambda_init = 0.8 - 0.6 * math.exp(-0.3 * l)
        xf, xb = _ffn_ln(xf, bf(ffn1_w_gate[l]), bf(ffn1_w_up[l]), bf(ffn1_w_down[l]),
                         vec(ln_g[l, 0]), vec(ln_b[l, 0]), alpha=alpha, tm=tm, tf=tf)

        hall = _in_proj(xb, _arrange_w_in(w_in[l]), vec(dsa_kv_g[l]), tm=_pick(n, 1024))
        ya = _diff_attn(hall.reshape(bsz, seq // tq_diff, tq_diff, IN_PADDED), _alibi_slopes(DIFF_HEADS),
                        diff_lambda[l], vec(diff_subln_g[l]), tq=tq_diff, lambda_init=lambda_init)
        yb = _dsa_attn(hall.reshape(bsz, seq // tq_dsa, tq_dsa, IN_PADDED), _alibi_slopes(DSA_HEADS),
                       bf(jnp.swapaxes(dsa_w_uk[l], 1, 2)), bf(dsa_w_uv[l]), tq=tq_dsa, top_k=top_k)
        xf, xb = _mix_ln(xb, xf, ya.reshape(n, -1), yb.reshape(n, -1), bf(w_gate[l]), b_gate[l],
                         bf(w_branch_a[l]), bf(w_branch_b[l]), bf(w_mix_out[l]),
                         vec(ln_g[l, 1]), vec(ln_b[l, 1]), alpha=alpha, tm=tm, tn=_pick(d, 512))

        kv = _mem_kv(mem, bf(mem_w_kv[l]))
        x3, xb3 = _mem_ln(xb.reshape(bsz, seq, d), xf.reshape(bsz, seq, d), kv, bf(mem_w_q[l]), bf(mem_w_o[l]),
                          vec(ln_g[l, 2]), vec(ln_b[l, 2]), alpha=alpha, tm=tm_mem)
        xf = x3.reshape(n, d)

        xf, xb = _ffn_ln(xf, bf(ffn2_w_gate[l]), bf(ffn2_w_up[l]), bf(ffn2_w_down[l]),
                         vec(ln_g[l, 3]), vec(ln_b[l, 3]), alpha=alpha, tm=tm, tf=tf)
    return xf.reshape(bsz, seq, d)
```
